```python
import jax, jax.numpy as jnp
from jax import lax
import numpy as np

D_MODEL = 1024
BATCH = 1
SEQ = 16384
DEPTH = 1
DEC_BATCH = 128
DEC_SEQ = 4
PAST_LEN = 8192
PAGE_SIZE = 128

N_HEADS_A = D_MODEL // 128
HEAD_DIM_A = 64
ATTN_WIDTH = N_HEADS_A * HEAD_DIM_A
DILATED_GROUPS = ((128, 1), (512, 4), (2048, 16))
WINDOW = 2048
N_HEADS_B = D_MODEL // 256
DK_B = 128
DV_B = 128
HK_B = N_HEADS_B * DK_B
HGRN_WIDTH = N_HEADS_B * DV_B
CHUNK_B = 64
MIX_WIDTH = ATTN_WIDTH + HGRN_WIDTH
SPLIT_POINTS = (ATTN_WIDTH, 2 * ATTN_WIDTH, 3 * ATTN_WIDTH, 3 * ATTN_WIDTH + HK_B,
                3 * ATTN_WIDTH + 2 * HK_B, 3 * ATTN_WIDTH + 2 * HK_B + HGRN_WIDTH)
IN_COLS = 3 * ATTN_WIDTH + 2 * HK_B + 2 * HGRN_WIDTH
D_FF = 256 * ((8 * D_MODEL // 3 + 255) // 256)
EPS = 1e-6

kernel_name = "hybrid_dilated_attn_hgrn2_macaron_step"


def rms_norm(x, g):
    xf = x.astype(jnp.float32)
    y = xf * lax.rsqrt(jnp.mean(xf * xf, axis=-1, keepdims=True) + EPS)
    return (y * g.astype(jnp.float32)).astype(x.dtype)


def alibi_slopes(n):
    return jnp.asarray([2.0 ** (-8.0 * (h + 1) / n) for h in range(n)], jnp.float32)


def swiglu(h, w_gate_up, w_down):
    g, u = jnp.split(h @ w_gate_up, 2, axis=-1)
    return (jax.nn.silu(g) * u) @ w_down


def dilated_attn_prompt(q, k, v, window, dil, slopes):
    B, S, H, E = q.shape
    L = window // dil
    Sp = -(-S // window) * window
    n = Sp // dil
    nb = n // L

    def to_blocks(t):
        t = jnp.pad(t, ((0, 0), (0, Sp - S), (0, 0), (0, 0))).reshape(B, n, dil, H, E)
        return t.transpose(0, 2, 1, 3, 4).reshape(B, dil, nb, L, H, E)

    def with_prev(t):
        prev = jnp.pad(t, ((0, 0), (0, 0), (1, 0), (0, 0), (0, 0), (0, 0)))[:, :, :-1]
        return jnp.concatenate([prev, t], axis=3)

    qb = to_blocks(q)
    kk = with_prev(to_blocks(k))
    vv = with_prev(to_blocks(v))
    s = jnp.einsum('brnqhe,brnkhe->brnhqk', qb, kk, preferred_element_type=jnp.float32)
    qi = jnp.arange(L)[:, None]
    ki = jnp.arange(2 * L)[None, :]
    steps = qi + L - ki
    band = (steps >= 0) & (steps <= L)
    first = (jnp.arange(nb)[:, None, None] == 0) & (ki < L)[None]
    valid = band[None] & ~first
    dist = (steps * dil).astype(jnp.float32)
    s = s - slopes[:, None, None] * dist
    s = jnp.where(valid[:, None], s, -jnp.inf)
    lse = jax.nn.logsumexp(s, axis=-1)
    p = jnp.exp(s - lse[..., None])
    o = jnp.einsum('brnhqk,brnkhe->brnqhe', p, vv.astype(jnp.float32))
    o = o.reshape(B, dil, n, H, E).transpose(0, 2, 1, 3, 4).reshape(B, Sp, H, E)[:, :S]
    lse = lse.transpose(0, 1, 2, 4, 3).reshape(B, dil, n, H).transpose(0, 2, 1, 3).reshape(B, Sp, H)[:, :S]
    return o, lse


def dilated_attn_sample(q, kcat, vcat, window, dil, slopes):
    T = q.shape[1]
    P = kcat.shape[1] - T
    j = jnp.arange(window // dil + 1)
    idx = P + jnp.arange(T)[:, None] - j[None, :] * dil
    valid = idx >= 0
    idxc = jnp.maximum(idx, 0)
    kg = kcat[:, idxc]
    vg = vcat[:, idxc]
    s = jnp.einsum('bthe,btjhe->bhtj', q, kg, preferred_element_type=jnp.float32)
    s = s - slopes[:, None, None] * (j * dil).astype(jnp.float32)[None, None, :]
    s = jnp.where(valid[None, None], s, -jnp.inf)
    lse = jax.nn.logsumexp(s, axis=-1)
    p = jnp.exp(s - lse[..., None])
    o = jnp.einsum('bhtj,btjhe->bthe', p, vg.astype(jnp.float32))
    return o, lse.transpose(0, 2, 1)


def merge_by_denominator(outs, lses):
    w = jax.nn.softmax(jnp.stack(lses, 0), axis=0)
    return jnp.einsum('gbsh,gbshe->bshe', w, jnp.stack(outs, 0))


def hgrn_chunk(state, q, k, v, logf):
    C = q.shape[1]
    b = jnp.cumsum(logf, axis=1)
    causal = jnp.tril(jnp.ones((C, C), bool))[None, :, :, None, None]
    rel = b[:, :, None] - b[:, None, :]
    decay = jnp.exp(jnp.where(causal, rel, -jnp.inf))
    a = jnp.einsum('bthk,btshk,bshk->bhts', q, decay, k)
    o = jnp.einsum('bhts,bshv->bthv', a, v)
    o = o + jnp.einsum('bthk,bhkv->bthv', q * jnp.exp(b), state)
    b_last = b[:, -1]
    new_state = jnp.exp(b_last)[..., None] * state + jnp.einsum(
        'bshk,bshv->bhkv', k * jnp.exp(b_last[:, None] - b), v)
    return o, new_state


def hgrn_prompt(q, k, v, logf):
    B, S, H, _ = q.shape
    nc = S // CHUNK_B

    def split(t):
        return t.reshape(B, nc, CHUNK_B, H, t.shape[-1]).swapaxes(0, 1)

    def step(st, xs):
        o, st = hgrn_chunk(st, *xs)
        return st, o

    init = jnp.zeros((B, H, DK_B, DV_B), jnp.float32)
    st, o = lax.scan(step, init, (split(q), split(k), split(v), split(logf)))
    return o.swapaxes(0, 1).reshape(B, S, H, DV_B), st


def mixer_inputs(h, w_in, q_norm, k_norm, lb):
    B, S, _ = h.shape
    qa, ka, va, qb, fb, ib, gb = jnp.split(h @ w_in, SPLIT_POINTS, axis=-1)
    qa = rms_norm(qa.reshape(B, S, N_HEADS_A, HEAD_DIM_A), q_norm) * (HEAD_DIM_A ** -0.5)
    ka = rms_norm(ka.reshape(B, S, N_HEADS_A, HEAD_DIM_A), k_norm)
    va = va.reshape(B, S, N_HEADS_A, HEAD_DIM_A)
    f = lb + (1.0 - lb) * jax.nn.sigmoid(fb.astype(jnp.float32))
    logf = jnp.log(f).reshape(B, S, N_HEADS_B, DK_B)
    kb = (1.0 - f).reshape(B, S, N_HEADS_B, DK_B)
    qb = jax.nn.silu(qb.astype(jnp.float32)).reshape(B, S, N_HEADS_B, DK_B)
    vb = ib.astype(jnp.float32).reshape(B, S, N_HEADS_B, DV_B)
    gb = gb.reshape(B, S, N_HEADS_B, DV_B)
    return (qa, ka, va), (qb, kb, vb, logf), gb


def mixer_output(attn_o, hgrn_o, g, attn_out_norm, hgrn_out_norm, w_out, dtype):
    B, S = attn_o.shape[:2]
    a = rms_norm(attn_o.reshape(B, S, ATTN_WIDTH), attn_out_norm)
    r = rms_norm(hgrn_o, hgrn_out_norm.reshape(N_HEADS_B, DV_B)) * jax.nn.silu(g.astype(jnp.float32))
    return jnp.concatenate([a, r.reshape(B, S, HGRN_WIDTH)], axis=-1).astype(dtype) @ w_out


def mixer_prompt(h, w_in, q_norm, k_norm, lb, attn_out_norm, hgrn_out_norm, w_out):
    S = h.shape[1]
    (qa, ka, va), (qb, kb, vb, logf), g = mixer_inputs(h, w_in, q_norm, k_norm, lb)
    slopes = alibi_slopes(N_HEADS_A)
    outs, lses = [], []
    for window, dil in DILATED_GROUPS:
        o, lse = dilated_attn_prompt(qa, ka, va, window, dil, slopes)
        outs.append(o)
        lses.append(lse)
    attn_o = merge_by_denominator(outs, lses)
    hgrn_o, st = hgrn_prompt(qb, kb, vb, logf)
    y = mixer_output(attn_o, hgrn_o, g, attn_out_norm, hgrn_out_norm, w_out, h.dtype)
    keep = min(WINDOW, S)
    return y, ka[:, S - keep:], va[:, S - keep:], st


def mixer_sample(h, ck, cv, st, w_in, q_norm, k_norm, lb, attn_out_norm, hgrn_out_norm, w_out):
    (qa, ka, va), (qb, kb, vb, logf), g = mixer_inputs(h, w_in, q_norm, k_norm, lb)
    slopes = alibi_slopes(N_HEADS_A)
    kcat = jnp.concatenate([ck.astype(ka.dtype), ka], axis=1)
    vcat = jnp.concatenate([cv.astype(va.dtype), va], axis=1)
    outs, lses = [], []
    for window, dil in DILATED_GROUPS:
        o, lse = dilated_attn_sample(qa, kcat, vcat, window, dil, slopes)
        outs.append(o)
        lses.append(lse)
    attn_o = merge_by_denominator(outs, lses)
    hgrn_o, new_st = hgrn_chunk(st.astype(jnp.float32), qb, kb, vb, logf)
    y = mixer_output(attn_o, hgrn_o, g, attn_out_norm, hgrn_out_norm, w_out, h.dtype)
    return y, ka, va, new_st


def setup_inputs(seed: int = 0) -> dict:
    key = jax.random.key(seed)
    ks = jax.random.split(key, 20)
    f32 = jnp.float32
    wb = min(WINDOW, PAST_LEN)

    def nrm(k, shape, scale):
        return scale * jax.random.normal(k, shape, f32)

    def gain(k, shape):
        return 1.0 + 0.05 * jax.random.normal(k, shape, f32)

    return {
        "x_prompt": nrm(ks[0], (BATCH, SEQ, D_MODEL), 1.0),
        "x_sample": nrm(ks[1], (DEC_BATCH, DEC_SEQ, D_MODEL), 1.0),
        "cache_k": nrm(ks[2], (DEPTH, DEC_BATCH, wb, N_HEADS_A, HEAD_DIM_A), 1.0),
        "cache_v": nrm(ks[3], (DEPTH, DEC_BATCH, wb, N_HEADS_A, HEAD_DIM_A), 1.0),
        "state_hgrn": nrm(ks[4], (DEPTH, DEC_BATCH, N_HEADS_B, DK_B, DV_B), 0.3),
        "norm_ffn1": gain(ks[5], (DEPTH, D_MODEL)),
        "ffn1_w_gate_up": nrm(ks[6], (DEPTH, D_MODEL, 2 * D_FF), D_MODEL ** -0.5),
        "ffn1_w_down": nrm(ks[7], (DEPTH, D_FF, D_MODEL), D_FF ** -0.5),
        "norm_mix": gain(ks[8], (DEPTH, D_MODEL)),
        "w_in": nrm(ks[9], (DEPTH, D_MODEL, IN_COLS), D_MODEL ** -0.5),
        "q_norm": gain(ks[10], (DEPTH, HEAD_DIM_A)),
        "k_norm": gain(ks[11], (DEPTH, HEAD_DIM_A)),
        "gamma_lb": nrm(ks[12], (DEPTH + 1, HK_B), 0.5),
        "attn_out_norm": gain(ks[13], (DEPTH, ATTN_WIDTH)),
        "hgrn_out_norm": gain(ks[14], (DEPTH, HGRN_WIDTH)),
        "w_out": nrm(ks[15], (DEPTH, MIX_WIDTH, D_MODEL), MIX_WIDTH ** -0.5),
        "norm_ffn2": gain(ks[16], (DEPTH, D_MODEL)),
        "ffn2_w_gate_up": nrm(ks[17], (DEPTH, D_MODEL, 2 * D_FF), D_MODEL ** -0.5),
        "ffn2_w_down": nrm(ks[18], (DEPTH, D_FF, D_MODEL), D_FF ** -0.5),
    }


def reference(x_prompt, x_sample, cache_k, cache_v, state_hgrn, norm_ffn1, ffn1_w_gate_up,
              ffn1_w_down, norm_mix, w_in, q_norm, k_norm, gamma_lb, attn_out_norm,
              hgrn_out_norm, w_out, norm_ffn2, ffn2_w_gate_up, ffn2_w_down):
    lb_all = jnp.cumsum(jax.nn.softmax(gamma_lb.astype(jnp.float32), axis=0), axis=0)
    hp, hs = x_prompt, x_sample
    kp_l, vp_l, sp_l, ks_l, vs_l, ss_l = [], [], [], [], [], []
    for l in range(DEPTH):
        mp = (w_in[l], q_norm[l], k_norm[l], lb_all[l], attn_out_norm[l], hgrn_out_norm[l], w_out[l])
        hp = hp + 0.5 * swiglu(rms_norm(hp, norm_ffn1[l]), ffn1_w_gate_up[l], ffn1_w_down[l])
        hs = hs + 0.5 * swiglu(rms_norm(hs, norm_ffn1[l]), ffn1_w_gate_up[l], ffn1_w_down[l])
        yp, kp, vp, sp = mixer_prompt(rms_norm(hp, norm_mix[l]), *mp)
        ys, kn, vn, sn = mixer_sample(rms_norm(hs, norm_mix[l]), cache_k[l], cache_v[l], state_hgrn[l], *mp)
        hp = hp + yp
        hs = hs + ys
        hp = hp + 0.5 * swiglu(rms_norm(hp, norm_ffn2[l]), ffn2_w_gate_up[l], ffn2_w_down[l])
        hs = hs + 0.5 * swiglu(rms_norm(hs, norm_ffn2[l]), ffn2_w_gate_up[l], ffn2_w_down[l])
        kp_l.append(kp); vp_l.append(vp); sp_l.append(sp)
        ks_l.append(kn); vs_l.append(vn); ss_l.append(sn)
    return (hp, hs, jnp.stack(kp_l), jnp.stack(vp_l), jnp.stack(sp_l),
            jnp.stack(ks_l), jnp.stack(vs_l), jnp.stack(ss_l))
```

```python
import functools

import numpy as np
import jax
import jax.numpy as jnp
from jax import lax
from jax.experimental import pallas as pl
from jax.experimental.pallas import tpu as pltpu

F32 = jnp.float32
BF16 = jnp.bfloat16
EPS = 1e-6
MASKED = -1e30

HEAD_DIM_A = 64
HEAD_DIM_B = 128
GROUP = 512
N_HEADS_A = GROUP // HEAD_DIM_A
N_HEADS_B = GROUP // HEAD_DIM_B
BAND = 128
DILATIONS = (1, 4, 16)
LANES = 128
N_PAIRS = GROUP // LANES
VMEM_LIMIT = 56 * 1024 * 1024


def _cparams(n_axes):
    return pltpu.CompilerParams(dimension_semantics=("arbitrary",) * n_axes,
                                vmem_limit_bytes=VMEM_LIMIT)


def _const_spec(shape):
    return pl.BlockSpec(shape, lambda *_: (0,) * len(shape))


def _dot(a, b):
    return jnp.dot(a, b, preferred_element_type=F32)


def _dot_nt(a, b):
    return lax.dot_general(a, b, (((1,), (1,)), ((), ())), preferred_element_type=F32)


def _rms_norm_rows(x, gain):
    return x * lax.rsqrt(jnp.mean(x * x, axis=-1, keepdims=True) + EPS) * gain


def _silu(x):
    return x * jax.nn.sigmoid(x)


def _alibi_slopes(n):
    return [2.0 ** (-8.0 * (h + 1) / n) for h in range(n)]


def _ffn_body(x_ref, g_ref, wgu_ref, wd_ref, o_ref, *, d_ff, f_chunk):
    x = x_ref[...]
    xn = _rms_norm_rows(x, g_ref[...]).astype(BF16)
    acc = jnp.zeros_like(x)
    for c in range(d_ff // f_chunk):
        lo = c * f_chunk
        gate = _dot(xn, wgu_ref[:, lo:lo + f_chunk])
        up = _dot(xn, wgu_ref[:, d_ff + lo:d_ff + lo + f_chunk])
        act = (_silu(gate) * up).astype(BF16)
        acc = acc + _dot(act, wd_ref[lo:lo + f_chunk, :])
    o_ref[...] = x + 0.5 * acc


def _ffn(x, gain, w_gate_up, w_down, *, row_tile, f_chunk=256):
    rows, d = x.shape
    d_ff = w_down.shape[0]
    assert rows % row_tile == 0 and d_ff % f_chunk == 0
    row_spec = pl.BlockSpec((row_tile, d), lambda i: (i, 0))
    return pl.pallas_call(
        functools.partial(_ffn_body, d_ff=d_ff, f_chunk=f_chunk),
        grid=(rows // row_tile,),
        in_specs=[row_spec, _const_spec((1, d)), _const_spec((d, 2 * d_ff)), _const_spec((d_ff, d))],
        out_specs=row_spec,
        out_shape=jax.ShapeDtypeStruct((rows, d), F32),
        compiler_params=_cparams(1),
        name="ffn",
    )(x, gain, w_gate_up, w_down)


def _inproj_body(h_ref, g_ref, win_ref, qn_ref, kn_ref, glb_ref, seg_ref,
                 qa_ref, ka_ref, va_ref, qb_ref, kb_ref, lf_ref, vb_ref, gb_ref, *, pair_major):
    xn = _rms_norm_rows(h_ref[...], g_ref[...]).astype(BF16)

    def proj(i):
        return _dot(xn, win_ref[:, i * GROUP:(i + 1) * GROUP])

    seg = seg_ref[...]

    def head_norm(u, gain):
        ms = _dot((u * u).astype(BF16), seg) * (1.0 / HEAD_DIM_A)
        return u * lax.rsqrt(ms + EPS) * gain

    def put(ref, val):
        if pair_major:
            for pair in range(N_PAIRS):
                ref[pair] = val[:, pair * LANES:(pair + 1) * LANES]
        else:
            ref[...] = val

    put(qa_ref, head_norm(proj(0), qn_ref[...]) * (HEAD_DIM_A ** -0.5))
    put(ka_ref, head_norm(proj(1), kn_ref[...]))
    put(va_ref, proj(2))
    qb_ref[...] = _silu(proj(3))
    g0 = glb_ref[0:1, :]
    g1 = glb_ref[1:2, :]
    gm = jnp.maximum(g0, g1)
    e0 = jnp.exp(g0 - gm)
    e1 = jnp.exp(g1 - gm)
    lb = e0 / (e0 + e1)
    f = lb + (1.0 - lb) * jax.nn.sigmoid(proj(4))
    lf_ref[...] = jnp.log(f)
    kb_ref[...] = 1.0 - f
    vb_ref[...] = proj(5)
    gb_ref[...] = proj(6)


def _inproj(h, gain, w_in, q_norm, k_norm, gamma_lb, *, row_tile, pair_major):
    rows, d = h.shape
    assert rows % row_tile == 0 and w_in.shape[1] == 7 * GROUP
    seg = np.kron(np.eye(N_HEADS_A), np.ones((HEAD_DIM_A, HEAD_DIM_A))).astype(np.float32)
    row_spec = pl.BlockSpec((row_tile, d), lambda i: (i, 0))
    out_spec = pl.BlockSpec((row_tile, GROUP), lambda i: (i, 0))
    out_shape = jax.ShapeDtypeStruct((rows, GROUP), F32)
    if pair_major:
        a_spec = pl.BlockSpec((N_PAIRS, row_tile, LANES), lambda i: (0, i, 0))
        a_shape = jax.ShapeDtypeStruct((N_PAIRS, rows, LANES), F32)
    else:
        a_spec, a_shape = out_spec, out_shape
    return pl.pallas_call(
        functools.partial(_inproj_body, pair_major=pair_major),
        grid=(rows // row_tile,),
        in_specs=[row_spec, _const_spec((1, d)), _const_spec((d, 7 * GROUP)),
                  _const_spec((1, GROUP)), _const_spec((1, GROUP)), _const_spec((2, GROUP)),
                  _const_spec((GROUP, GROUP))],
        out_specs=[a_spec] * 3 + [out_spec] * 5,
        out_shape=[a_shape] * 3 + [out_shape] * 5,
        compiler_params=_cparams(1),
        name="inproj",
    )(h, gain, w_in, jnp.tile(q_norm, N_HEADS_A)[None], jnp.tile(k_norm, N_HEADS_A)[None],
      gamma_lb, jnp.asarray(seg, BF16))


TILE = BAND * max(DILATIONS)


def _attn_prompt_body(q_ref, k_ref, v_ref, bias_ref, o_ref, kbuf, vbuf, lse_scr):
    tile = pl.program_id(1)

    @pl.when(tile == 0)
    def _():
        kbuf[0:TILE, :] = jnp.zeros((TILE, LANES), F32)
        vbuf[0:TILE, :] = jnp.zeros((TILE, LANES), F32)

    kbuf[TILE:2 * TILE, :] = k_ref[0]
    vbuf[TILE:2 * TILE, :] = v_ref[0]
    low_head = lax.broadcasted_iota(jnp.int32, (BAND, LANES), 1) < HEAD_DIM_A

    for g, dil in enumerate(DILATIONS):
        def unit(u, carry, g=g, dil=dil):
            r = u % dil
            n = u // dil
            q_start = BAND * dil * n + r
            kv_start = TILE + BAND * dil * (n - 1) + r
            if dil == 1:
                rows = pl.ds(pl.multiple_of(q_start, BAND), BAND)
                kv_rows = pl.ds(pl.multiple_of(kv_start, BAND), 2 * BAND)
            else:
                rows = pl.ds(q_start, BAND, stride=dil)
                kv_rows = pl.ds(kv_start, 2 * BAND, stride=dil)
            q2 = q_ref[0, rows, :]
            qs = jnp.concatenate([jnp.where(low_head, q2, 0.0), jnp.where(low_head, 0.0, q2)], axis=0)
            s = _dot_nt(qs.astype(BF16), kbuf[kv_rows, :].astype(BF16)) + bias_ref[0, g]
            penalty = jnp.where(jnp.logical_and(tile == 0, n == 0), MASKED, 0.0)
            s = jnp.concatenate([s[:, :BAND] + penalty, s[:, BAND:]], axis=1)
            m = jnp.max(s, axis=-1, keepdims=True)
            p = jnp.exp(s - m)
            l = jnp.sum(p, axis=-1, keepdims=True)
            pv = _dot(p.astype(BF16), vbuf[kv_rows, :].astype(BF16)) / l
            lse_rows = m + jnp.log(l)
            o = jnp.where(low_head, pv[:BAND], pv[BAND:])
            lse = jnp.where(low_head, lse_rows[:BAND], lse_rows[BAND:])
            if g == 0:
                o_ref[0, rows, :] = o
                lse_scr[rows, :] = lse
            else:
                o_old = o_ref[0, rows, :]
                lse_old = lse_scr[rows, :]
                top = jnp.maximum(lse_old, lse)
                w_old = jnp.exp(lse_old - top)
                w_new = jnp.exp(lse - top)
                total = w_old + w_new
                o_ref[0, rows, :] = (w_old * o_old + w_new * o) / total
                if g + 1 < len(DILATIONS):
                    lse_scr[rows, :] = top + jnp.log(total)
            return carry

        lax.fori_loop(0, TILE // BAND, unit, 0)

    kbuf[0:TILE, :] = k_ref[0]
    vbuf[0:TILE, :] = v_ref[0]


def _prompt_bias():
    qi = np.arange(BAND)[:, None]
    ki = np.arange(2 * BAND)[None, :]
    steps = qi + BAND - ki
    band = (steps >= 0) & (steps <= BAND)
    slopes = _alibi_slopes(N_HEADS_A)
    out = np.empty((N_PAIRS, len(DILATIONS), 2 * BAND, 2 * BAND), np.float32)
    for pair in range(N_PAIRS):
        for g, dil in enumerate(DILATIONS):
            for sub in range(2):
                out[pair, g, sub * BAND:(sub + 1) * BAND] = np.where(band, -slopes[2 * pair + sub] * steps * dil, MASKED)
    return out


def _attn_prompt(q, k, v):
    n_pairs, s_len, _ = q.shape
    assert n_pairs == N_PAIRS and s_len % TILE == 0
    blk = pl.BlockSpec((1, TILE, LANES), lambda p, t: (p, t, 0))
    bias = _prompt_bias()
    return pl.pallas_call(
        _attn_prompt_body,
        grid=(N_PAIRS, s_len // TILE),
        in_specs=[blk, blk, blk, pl.BlockSpec((1,) + bias.shape[1:], lambda p, t: (p, 0, 0, 0))],
        out_specs=blk,
        out_shape=jax.ShapeDtypeStruct(q.shape, F32),
        scratch_shapes=[pltpu.VMEM((2 * TILE, LANES), F32), pltpu.VMEM((2 * TILE, LANES), F32),
                        pltpu.VMEM((TILE, LANES), F32)],
        compiler_params=_cparams(2),
        name="attn_prompt",
    )(q, k, v, jnp.asarray(bias))


CHUNK = 128
N_LEVELS = 7


def _hgrn_tables():
    c = CHUNK
    t_op = np.zeros((N_LEVELS + 2, c, c), np.float32)
    idx = np.arange(c)
    for lvl in range(N_LEVELS):
        m = 1 << lvl
        for t in range(c):
            blk = t // m
            if blk % 2 == 1:
                t_op[lvl, t, blk * m:t + 1] = 1.0
            else:
                t_op[lvl, t, t + 1:(blk + 1) * m] = 1.0
    t_op[N_LEVELS] = (idx[None, :] <= idx[:, None])
    t_op[N_LEVELS + 1] = (idx[None, :] > idx[:, None])
    x = idx[:, None] ^ idx[None, :]
    level = np.where(x > 0, np.floor(np.log2(np.maximum(x, 1))), -1).astype(np.int32)
    level = np.where(idx[:, None] < idx[None, :], -2, level).astype(np.int32)
    return t_op.reshape((N_LEVELS + 2) * c, c), level


def _hgrn_prompt_body(q_ref, k_ref, v_ref, lf_ref, top_ref, lvl_ref, o_ref, st_ref, st_scr):
    step = pl.program_id(0)

    @pl.when(step == 0)
    def _():
        st_scr[...] = jnp.zeros_like(st_scr)

    c = CHUNK
    t_op = top_ref[...]
    level = lvl_ref[...]
    row = lax.broadcasted_iota(jnp.int32, (c, HEAD_DIM_B), 0)
    for h in range(N_HEADS_B):
        sl = slice(h * HEAD_DIM_B, (h + 1) * HEAD_DIM_B)
        q = q_ref[:, sl]
        k = k_ref[:, sl]
        v = v_ref[:, sl]
        lf = lf_ref[:, sl]
        hi = lf.astype(BF16)
        rem = lf - hi.astype(F32)
        mid = rem.astype(BF16)
        low = (rem - mid.astype(F32)).astype(BF16)
        e = jnp.exp(_dot(t_op, hi) + _dot(t_op, mid) + _dot(t_op, low))
        a = jnp.where(level == -1, _dot_nt(q.astype(BF16), k.astype(BF16)), 0.0)
        for lvl in range(N_LEVELS):
            odd = ((row >> lvl) & 1) == 1
            x = (jnp.where(odd, q, k) * e[lvl * c:(lvl + 1) * c]).astype(BF16)
            a = jnp.where(level == lvl, _dot_nt(x, x), a)
        st = st_scr[h]
        q_in = (q * e[N_LEVELS * c:(N_LEVELS + 1) * c]).astype(BF16)
        o_ref[:, sl] = _dot(a.astype(BF16), v.astype(BF16)) + _dot_nt(q_in, st.astype(BF16))
        k_out = (k * e[(N_LEVELS + 1) * c:(N_LEVELS + 2) * c]).astype(BF16)
        decay = e[(N_LEVELS + 1) * c - 1:(N_LEVELS + 1) * c]
        st_new = st * decay + _dot(v.T.astype(BF16), k_out)
        st_scr[h] = st_new

        @pl.when(step == pl.num_programs(0) - 1)
        def _():
            st_ref[h] = st_new.T


def _hgrn_prompt(q, k, v, logf):
    s_len = q.shape[0]
    assert s_len % CHUNK == 0
    t_op, level = _hgrn_tables()
    blk = pl.BlockSpec((CHUNK, GROUP), lambda i: (i, 0))
    return pl.pallas_call(
        _hgrn_prompt_body,
        grid=(s_len // CHUNK,),
        in_specs=[blk, blk, blk, blk, _const_spec(t_op.shape), _const_spec(level.shape)],
        out_specs=[blk, _const_spec((N_HEADS_B, HEAD_DIM_B, HEAD_DIM_B))],
        out_shape=[jax.ShapeDtypeStruct((s_len, GROUP), F32),
                   jax.ShapeDtypeStruct((N_HEADS_B, HEAD_DIM_B, HEAD_DIM_B), F32)],
        scratch_shapes=[pltpu.VMEM((N_HEADS_B, HEAD_DIM_B, HEAD_DIM_B), F32)],
        compiler_params=_cparams(1),
        name="hgrn_prompt",
    )(q, k, v, logf, jnp.asarray(t_op, BF16), jnp.asarray(level))


HEADS_PER_SLAB = 4
SLAB = HEADS_PER_SLAB * HEAD_DIM_A


def _attn_sample_body(q_ref, kn_ref, vn_ref, kt_ref, vt_ref, bias_ref, biasn_ref, o_ref, *, t_new):
    n_rows = HEADS_PER_SLAB * t_new
    row_id = lax.broadcasted_iota(jnp.int32, (n_rows, SLAB), 0)
    lane_id = lax.broadcasted_iota(jnp.int32, (n_rows, SLAB), 1)
    own = (row_id // t_new) == (lane_id // HEAD_DIM_A)
    for slab in range(GROUP // SLAB):
        cs = slice(slab * SLAB, (slab + 1) * SLAB)
        q_bd = jnp.where(own, q_ref[0, :, cs], 0.0)
        s_all = _dot(q_bd.astype(BF16), kt_ref[0, cs, :].astype(BF16))
        k_new = kn_ref[0, :, cs]
        v_new = vn_ref[0, :, cs]
        s_new = [jnp.sum(q_bd * k_new[t2:t2 + 1, :], axis=-1, keepdims=True) for t2 in range(t_new)]
        stats = []
        for g in range(len(DILATIONS)):
            s = s_all + bias_ref[slab, g]
            bn = biasn_ref[slab, g]
            sn = [s_new[t2] + bn[:, t2:t2 + 1] for t2 in range(t_new)]
            m = functools.reduce(jnp.maximum, sn, jnp.max(s, axis=-1, keepdims=True))
            p = jnp.exp(s - m)
            pn = [jnp.exp(x - m) for x in sn]
            stats.append((m, jnp.sum(p, axis=-1, keepdims=True) + sum(pn), p, pn))
        m_all = functools.reduce(jnp.maximum, [st[0] for st in stats])
        l_all = sum(st[1] * jnp.exp(st[0] - m_all) for st in stats)
        scales = [jnp.exp(st[0] - m_all) / l_all for st in stats]
        p_mix = sum(sc * st[2] for sc, st in zip(scales, stats))
        o = _dot_nt(p_mix.astype(BF16), vt_ref[0, cs, :].astype(BF16))
        for t2 in range(t_new):
            o = o + sum(sc * st[3][t2] for sc, st in zip(scales, stats)) * v_new[t2:t2 + 1, :]
        o = jnp.where(own, o, 0.0)
        o_ref[0, :, cs] = sum(o[h * t_new:(h + 1) * t_new] for h in range(HEADS_PER_SLAB))


def _sample_bias(past, t_new):
    slopes = _alibi_slopes(N_HEADS_A)
    n_slabs = GROUP // SLAB
    n_rows = HEADS_PER_SLAB * t_new
    bias = np.full((n_slabs, len(DILATIONS), n_rows, past), MASKED, np.float32)
    bias_new = np.full((n_slabs, len(DILATIONS), n_rows, LANES), MASKED, np.float32)
    pos = np.arange(past)
    for slab in range(n_slabs):
        for g, dil in enumerate(DILATIONS):
            for h in range(HEADS_PER_SLAB):
                slope = slopes[slab * HEADS_PER_SLAB + h]
                for t in range(t_new):
                    d = past + t - pos
                    ok = (d % dil == 0) & (d // dil >= 1) & (d // dil <= BAND)
                    bias[slab, g, h * t_new + t] = np.where(ok, -slope * d, MASKED)
                    for t2 in range(t + 1):
                        if (t - t2) % dil == 0:
                            bias_new[slab, g, h * t_new + t, t2] = -slope * (t - t2)
    return bias, bias_new


def _attn_sample(q, k_new, v_new, cache_kt, cache_vt):
    b, t_new, _ = q.shape
    past = cache_kt.shape[2]
    assert t_new <= LANES and past % LANES == 0
    n_rows = HEADS_PER_SLAB * t_new
    bias, bias_new = _sample_bias(past, t_new)
    new_spec = pl.BlockSpec((1, t_new, GROUP), lambda i: (i, 0, 0))
    cache_spec = pl.BlockSpec((1, GROUP, past), lambda i: (i, 0, 0))
    return pl.pallas_call(
        functools.partial(_attn_sample_body, t_new=t_new),
        grid=(b,),
        in_specs=[pl.BlockSpec((1, n_rows, GROUP), lambda i: (i, 0, 0)), new_spec, new_spec,
                  cache_spec, cache_spec, _const_spec(bias.shape), _const_spec(bias_new.shape)],
        out_specs=new_spec,
        out_shape=jax.ShapeDtypeStruct((b, t_new, GROUP), F32),
        compiler_params=_cparams(1),
        name="attn_sample",
    )(jnp.tile(q, (1, HEADS_PER_SLAB, 1)), k_new, v_new, cache_kt, cache_vt,
      jnp.asarray(bias), jnp.asarray(bias_new))


def _hgrn_sample_body(q_ref, k_ref, v_ref, lf_ref, st_ref, ones_ref, o_ref, nst_ref,
                      qin_scr, kout_scr, v8_scr, *, t_new, b_blk):
    ones = ones_ref[...]
    q = [q_ref[:, t, :] for t in range(t_new)]
    k = [k_ref[:, t, :] for t in range(t_new)]
    v = [v_ref[:, t, :] for t in range(t_new)]
    lf = [lf_ref[:, t, :] for t in range(t_new)]
    zero = jnp.zeros_like(q[0])
    for t in range(t_new):
        acc = zero
        for s in range(t + 1):
            w = q[t] * k[s]
            if s < t:
                w = w * jnp.exp(sum(lf[s + 1:t + 1]))
            acc = acc + _dot(w.astype(BF16), ones) * v[s]
        o_ref[:, t, :] = acc
    for t in range(8):
        if t < t_new:
            qin_scr[:, t, :] = q[t] * jnp.exp(sum(lf[:t + 1]))
            kout_scr[:, t, :] = k[t] * jnp.exp(sum(lf[t + 1:])) if t + 1 < t_new else k[t]
            v8_scr[:, t, :] = v[t]
        else:
            qin_scr[:, t, :] = zero
            kout_scr[:, t, :] = jnp.exp(sum(lf)) if t == t_new else zero
            v8_scr[:, t, :] = zero

    def per_batch(i, carry):
        for h in range(N_HEADS_B):
            sl = slice(h * HEAD_DIM_B, (h + 1) * HEAD_DIM_B)
            st = st_ref[i, h]
            o_in = _dot(qin_scr[i, :, sl].astype(BF16), st.astype(BF16))
            o_ref[i, :, sl] = o_ref[i, :, sl] + o_in[0:t_new]
            r_t = kout_scr[i, :, sl].T
            upd = _dot(r_t.astype(BF16), v8_scr[i, :, sl].astype(BF16))
            nst_ref[i, h] = st * r_t[:, t_new:t_new + 1] + upd
        return carry

    lax.fori_loop(0, b_blk, per_batch, 0)


def _hgrn_sample(q, k, v, logf, state, *, b_blk=16):
    b, t_new, _ = q.shape
    assert b % b_blk == 0 and t_new < 8
    ones = np.kron(np.eye(N_HEADS_B), np.ones((HEAD_DIM_B, HEAD_DIM_B))).astype(np.float32)
    tok = pl.BlockSpec((b_blk, t_new, GROUP), lambda i: (i, 0, 0))
    st = pl.BlockSpec((b_blk, N_HEADS_B, HEAD_DIM_B, HEAD_DIM_B), lambda i: (i, 0, 0, 0))
    return pl.pallas_call(
        functools.partial(_hgrn_sample_body, t_new=t_new, b_blk=b_blk),
        grid=(b // b_blk,),
        in_specs=[tok, tok, tok, tok, st, _const_spec(ones.shape)],
        out_specs=[tok, st],
        out_shape=[jax.ShapeDtypeStruct(q.shape, F32), jax.ShapeDtypeStruct(state.shape, F32)],
        scratch_shapes=[pltpu.VMEM((b_blk, 8, GROUP), F32)] * 3,
        compiler_params=_cparams(1),
        name="hgrn_sample",
    )(q, k, v, logf, state, jnp.asarray(ones, BF16))


def _outproj_body(h_ref, a_ref, hg_ref, gb_ref, an_ref, hn_ref, wout_ref, y_ref, *, pair_major):
    if pair_major:
        attn = jnp.concatenate([a_ref[pair] for pair in range(N_PAIRS)], axis=-1)
    else:
        attn = a_ref[...]
    a = _rms_norm_rows(attn, an_ref[...]).astype(BF16)
    y = _dot(a, wout_ref[0:GROUP, :])
    hg = hg_ref[...]
    gate = _silu(gb_ref[...])
    for hd in range(N_HEADS_B):
        sl = slice(hd * HEAD_DIM_B, (hd + 1) * HEAD_DIM_B)
        r = (_rms_norm_rows(hg[:, sl], hn_ref[:, sl]) * gate[:, sl]).astype(BF16)
        y = y + _dot(r, wout_ref[GROUP + hd * HEAD_DIM_B:GROUP + (hd + 1) * HEAD_DIM_B, :])
    y_ref[...] = h_ref[...] + y


def _outproj(h, attn, hgrn_o, gate, attn_norm, hgrn_norm, w_out, *, row_tile, pair_major):
    rows, d = h.shape
    assert rows % row_tile == 0
    row = lambda w: pl.BlockSpec((row_tile, w), lambda i: (i, 0))
    a_spec = pl.BlockSpec((N_PAIRS, row_tile, LANES), lambda i: (0, i, 0)) if pair_major else row(GROUP)
    return pl.pallas_call(
        functools.partial(_outproj_body, pair_major=pair_major),
        grid=(rows // row_tile,),
        in_specs=[row(d), a_spec, row(GROUP), row(GROUP), _const_spec((1, GROUP)), _const_spec((1, GROUP)),
                  _const_spec(w_out.shape)],
        out_specs=row(d),
        out_shape=jax.ShapeDtypeStruct((rows, d), F32),
        compiler_params=_cparams(1),
        name="outproj",
    )(h, attn, hgrn_o, gate, attn_norm[None], hgrn_norm[None], w_out)


def _layer(x_prompt, x_sample, cache_k, cache_v, state, p):
    batch, s_len, d = x_prompt.shape
    dec_b, t_new, _ = x_sample.shape
    assert batch == 1
    hp = x_prompt.reshape(s_len, d)
    hs = x_sample.reshape(dec_b * t_new, d)
    rt_p = 512
    rt_s = dec_b * t_new

    hp = _ffn(hp, p["norm_ffn1"], p["ffn1_gu"], p["ffn1_d"], row_tile=rt_p)
    hs = _ffn(hs, p["norm_ffn1"], p["ffn1_gu"], p["ffn1_d"], row_tile=rt_s)

    proj = functools.partial(_inproj, gain=p["norm_mix"], w_in=p["w_in"], q_norm=p["q_norm"],
                             k_norm=p["k_norm"], gamma_lb=p["gamma_lb"])
    qa, ka, va, qb, kb, lf, vb, gb = proj(hp, row_tile=rt_p, pair_major=True)
    attn_p = _attn_prompt(qa, ka, va)
    hg, st_p = _hgrn_prompt(qb, kb, vb, lf)
    hp = _outproj(hp, attn_p, hg, gb, p["attn_out_norm"], p["hgrn_out_norm"], p["w_out"],
                  row_tile=rt_p, pair_major=True)

    qa_s, ka_s, va_s, qb_s, kb_s, lf_s, vb_s, gb_s = proj(hs, row_tile=rt_s, pair_major=False)
    per_seq = lambda t: t.reshape(dec_b, t_new, GROUP)
    past = cache_k.shape[1]
    transposed = lambda c: jnp.transpose(c, (0, 2, 3, 1)).reshape(dec_b, GROUP, past)
    attn_s = _attn_sample(per_seq(qa_s), per_seq(ka_s), per_seq(va_s), transposed(cache_k), transposed(cache_v))
    hg_s, st_s = _hgrn_sample(per_seq(qb_s), per_seq(kb_s), per_seq(vb_s), per_seq(lf_s), state)
    hs = _outproj(hs, attn_s.reshape(dec_b * t_new, GROUP), hg_s.reshape(dec_b * t_new, GROUP), gb_s,
                  p["attn_out_norm"], p["hgrn_out_norm"], p["w_out"], row_tile=rt_s, pair_major=False)

    hp = _ffn(hp, p["norm_ffn2"], p["ffn2_gu"], p["ffn2_d"], row_tile=rt_p)
    hs = _ffn(hs, p["norm_ffn2"], p["ffn2_gu"], p["ffn2_d"], row_tile=rt_s)

    keep = min(TILE, s_len)
    heads = (N_HEADS_A, HEAD_DIM_A)
    last_rows = lambda t: jnp.transpose(t[:, s_len - keep:], (1, 0, 2)).reshape(batch, keep, *heads)
    return (hp.reshape(batch, s_len, d), hs.reshape(dec_b, t_new, d),
            last_rows(ka), last_rows(va), st_p[None],
            ka_s.reshape(dec_b, t_new, *heads), va_s.reshape(dec_b, t_new, *heads), st_s)


def kernel(x_prompt, x_sample, cache_k, cache_v, state_hgrn, norm_ffn1, ffn1_w_gate_up, ffn1_w_down, norm_mix, w_in, q_norm, k_norm, gamma_lb, attn_out_norm, hgrn_out_norm, w_out, norm_ffn2, ffn2_w_gate_up, ffn2_w_down):
    depth = w_in.shape[0]
    assert depth == 1 and gamma_lb.shape[0] == 2
    p = {
        "norm_ffn1": norm_ffn1[0][None], "ffn1_gu": ffn1_w_gate_up[0].astype(BF16), "ffn1_d": ffn1_w_down[0].astype(BF16),
        "norm_mix": norm_mix[0][None], "w_in": w_in[0].astype(BF16), "q_norm": q_norm[0], "k_norm": k_norm[0],
        "gamma_lb": gamma_lb, "attn_out_norm": attn_out_norm[0], "hgrn_out_norm": hgrn_out_norm[0],
        "w_out": w_out[0].astype(BF16),
        "norm_ffn2": norm_ffn2[0][None], "ffn2_gu": ffn2_w_gate_up[0].astype(BF16), "ffn2_d": ffn2_w_down[0].astype(BF16),
    }
    outs = _layer(x_prompt, x_sample, cache_k[0], cache_v[0], state_hgrn[0], p)
    return tuple(o[None] if i >= 2 else o for i, o in enumerate(outs))
```
